```python
import jax, jax.numpy as jnp
from jax import lax
import numpy as np

D_MODEL = 1024
BATCH = 4
SEQ = 8192
DEPTH = 1

MIX_WIDTH = D_MODEL
RET_HEADS = 4
RET_DK = MIX_WIDTH // 2 // RET_HEADS
RET_DV = MIX_WIDTH // 2 // RET_HEADS
RET_CHUNK = 128
ATT_HEADS = 4
ATT_DH = MIX_WIDTH // 2 // ATT_HEADS
DILATED_CONFIGS = ((128, 1), (512, 4), (2048, 16))
ATT_BLOCK = 128
ROPE_THETA = 10000.0
N_EXPERTS = 32
TOP_K = 4
D_EXPERT = D_MODEL
SWIGLU_LIMIT = 7.0
SWIGLU_ALPHA = 1.702
MOE_BLOCK = 128
NORM_EPS = 1e-5
IN_SIZES = (RET_HEADS * RET_DK, RET_HEADS * RET_DK, RET_HEADS * RET_DV, RET_HEADS * RET_DV,
            ATT_HEADS * ATT_DH, ATT_HEADS * ATT_DH, ATT_HEADS * ATT_DH)
D_IN = sum(IN_SIZES)
D_CAT = RET_HEADS * RET_DV + ATT_HEADS * ATT_DH

kernel_name = "hybrid_retention_dilated_moe_block"


def rms_norm(x, g):
    xf = x.astype(jnp.float32)
    y = xf * lax.rsqrt(jnp.mean(xf * xf, axis=-1, keepdims=True) + NORM_EPS)
    return (y * g.astype(jnp.float32)).astype(x.dtype)


def rotary(t, pos):
    dh = t.shape[-1]
    inv = ROPE_THETA ** (-jnp.arange(0, dh, 2, dtype=jnp.float32) / dh)
    ang = pos[:, None] * inv[None, :]
    cos = jnp.cos(ang)[None, :, None, :].astype(t.dtype)
    sin = jnp.sin(ang)[None, :, None, :].astype(t.dtype)
    t1, t2 = t[..., : dh // 2], t[..., dh // 2:]
    return jnp.concatenate([t1 * cos - t2 * sin, t1 * sin + t2 * cos], axis=-1)


def retention_chunkwise(q, k, v):
    B, S, H, Dk = q.shape
    Dv = v.shape[-1]
    C = RET_CHUNK
    N = S // C
    log_g = jnp.log1p(-(2.0 ** (-5.0 - jnp.arange(H, dtype=jnp.float32))))
    idx = jnp.arange(C, dtype=jnp.float32)
    diff = idx[:, None] - idx[None, :]
    intra_decay = jnp.where(diff[None] >= 0, jnp.exp(jnp.maximum(diff, 0.0)[None] * log_g[:, None, None]), 0.0)
    zeta = jnp.exp((C - 1 - idx)[None, :] * log_g[:, None])
    xi = jnp.exp((idx + 1)[None, :] * log_g[:, None])
    chunk_decay = jnp.exp(C * log_g)
    dt = q.dtype
    qc = q.reshape(B, N, C, H, Dk)
    kc = k.reshape(B, N, C, H, Dk)
    vc = v.reshape(B, N, C, H, Dv)
    s = jnp.einsum('bnihd,bnjhd->bnhij', qc, kc) * intra_decay.astype(dt)[None, None]
    intra = jnp.einsum('bnhij,bnjhe->bnihe', s, vc)
    kv = jnp.einsum('bnjhd,bnjhe,hj->nbhde', kc, vc, zeta.astype(dt))
    dec = chunk_decay.astype(dt)[None, :, None, None]

    def step(state, kv_n):
        return state * dec + kv_n, state

    _, state_prev = lax.scan(step, jnp.zeros((B, H, Dk, Dv), dt), kv)
    cross = jnp.einsum('bnihd,nbhde,hi->bnihe', qc, state_prev, xi.astype(dt))
    return (intra + cross).reshape(B, S, H, Dv)


def head_group_norm(o):
    of = o.astype(jnp.float32)
    mu = jnp.mean(of, axis=-1, keepdims=True)
    var = jnp.mean(jnp.square(of - mu), axis=-1, keepdims=True)
    return ((of - mu) * lax.rsqrt(var + NORM_EPS)).astype(o.dtype)


def dilated_branch(q, k, v, window, dilation):
    B, S, H, Dh = q.shape
    n_back = window // dilation
    L = S // dilation
    nb = -(-L // ATT_BLOCK)
    Lp = nb * ATT_BLOCK

    def to_sub(t):
        t = t.reshape(B, L, dilation, H, Dh).transpose(0, 2, 3, 1, 4)
        t = jnp.pad(t, ((0, 0), (0, 0), (0, 0), (0, Lp - L), (0, 0)))
        return t.reshape(B, dilation, H, nb, ATT_BLOCK, Dh)

    qs, ks, vs = to_sub(q), to_sub(k), to_sub(v)

    def with_prev(t):
        prev = jnp.concatenate([jnp.zeros_like(t[:, :, :, :1]), t[:, :, :, :-1]], axis=3)
        return jnp.concatenate([prev, t], axis=4)

    kk, vv = with_prev(ks), with_prev(vs)
    s = jnp.einsum('brhnqc,brhnkc->brhnqk', qs, kk).astype(jnp.float32)
    qi = jnp.arange(ATT_BLOCK)[:, None]
    kj = jnp.arange(2 * ATT_BLOCK)[None, :]
    dist = qi + ATT_BLOCK - kj
    band = (dist >= 0) & (dist <= n_back)
    blk = jnp.arange(nb)[:, None, None]
    valid = band[None] & ((kj[None] >= ATT_BLOCK) | (blk > 0))
    s = jnp.where(valid, s, -jnp.inf)
    m = jnp.max(s, axis=-1, keepdims=True)
    p = jnp.exp(s - m)
    den = jnp.sum(p, axis=-1, keepdims=True)
    o = jnp.einsum('brhnqk,brhnkc->brhnqc', p.astype(v.dtype), vv) / den.astype(v.dtype)
    lse = (m + jnp.log(den))[..., 0]
    o = o.reshape(B, dilation, H, Lp, Dh)[:, :, :, :L].transpose(0, 3, 1, 2, 4).reshape(B, S, H, Dh)
    lse = lse.reshape(B, dilation, H, Lp)[:, :, :, :L].transpose(0, 3, 1, 2).reshape(B, S, H)
    return o, lse


def dilated_mixture(q, k, v):
    outs, lses = [], []
    for window, dilation in DILATED_CONFIGS:
        o, lse = dilated_branch(q, k, v, window, dilation)
        outs.append(o)
        lses.append(lse)
    w = jax.nn.softmax(jnp.stack(lses, axis=0), axis=0)
    o = jnp.stack(outs, axis=0)
    return jnp.sum(w.astype(q.dtype)[..., None] * o, axis=0)


def hybrid_mixer(h, w_in, w_out, pos):
    B, S, _ = h.shape
    proj = h @ w_in
    offsets = list(np.cumsum(IN_SIZES)[:-1])
    rq, rk, rv, rg, aq, ak, av = jnp.split(proj, offsets, axis=-1)
    rq = rotary(rq.reshape(B, S, RET_HEADS, RET_DK), pos)
    rk = rotary(rk.reshape(B, S, RET_HEADS, RET_DK), pos) * (RET_DK ** -0.5)
    rv = rv.reshape(B, S, RET_HEADS, RET_DV)
    ret = head_group_norm(retention_chunkwise(rq, rk, rv)).reshape(B, S, RET_HEADS * RET_DV)
    ret = jax.nn.silu(rg) * ret
    aq = rotary(aq.reshape(B, S, ATT_HEADS, ATT_DH), pos) * (ATT_DH ** -0.5)
    ak = rotary(ak.reshape(B, S, ATT_HEADS, ATT_DH), pos)
    av = av.reshape(B, S, ATT_HEADS, ATT_DH)
    att = dilated_mixture(aq, ak, av).reshape(B, S, ATT_HEADS * ATT_DH)
    return jnp.concatenate([ret, att], axis=-1) @ w_out


def clamped_swiglu_expert(xb, w_gu, b_gu, w_dn, b_dn):
    gu = xb @ w_gu + b_gu
    x_glu, x_lin = gu[..., :D_EXPERT], gu[..., D_EXPERT:]
    x_glu = jnp.minimum(x_glu, SWIGLU_LIMIT)
    x_lin = jnp.clip(x_lin, -SWIGLU_LIMIT, SWIGLU_LIMIT)
    act = x_glu * jax.nn.sigmoid(SWIGLU_ALPHA * x_glu) * (x_lin + 1.0)
    return act @ w_dn + b_dn


def moe_ffn(h, w_router, b_router, w_gate_up, b_gate_up, w_down, b_down):
    B, S, D = h.shape
    T = B * S
    ht = h.reshape(T, D)
    logits = (ht @ w_router + b_router).astype(jnp.float32)
    top_val, top_idx = lax.top_k(logits, TOP_K)
    gates = jax.nn.softmax(top_val, axis=-1).astype(h.dtype)
    A = T * TOP_K
    e_flat = top_idx.reshape(A).astype(jnp.int32)
    g_flat = gates.reshape(A)
    tok_flat = jnp.arange(A, dtype=jnp.int32) // TOP_K
    order = jnp.argsort(e_flat, stable=True)
    e_sorted = e_flat[order]
    counts = jnp.bincount(e_flat, length=N_EXPERTS)
    padded = (counts + MOE_BLOCK - 1) // MOE_BLOCK * MOE_BLOCK
    start = jnp.cumsum(counts) - counts
    pend = jnp.cumsum(padded)
    pstart = pend - padded
    dest = pstart[e_sorted] + (jnp.arange(A, dtype=jnp.int32) - start[e_sorted])
    P = A + N_EXPERTS * MOE_BLOCK
    n_blk = P // MOE_BLOCK
    rows_tok = jnp.zeros((P,), jnp.int32).at[dest].set(tok_flat[order])
    rows_gate = jnp.zeros((P,), h.dtype).at[dest].set(g_flat[order])
    blk_expert = jnp.minimum(
        jnp.searchsorted(pend, jnp.arange(n_blk, dtype=jnp.int32) * MOE_BLOCK, side='right'),
        N_EXPERTS - 1).astype(jnp.int32)
    xb = ht[rows_tok].reshape(n_blk, MOE_BLOCK, D)

    def run_block(args):
        xblk, e = args
        return clamped_swiglu_expert(xblk, w_gate_up[e], b_gate_up[e], w_down[e], b_down[e])

    yb = lax.map(run_block, (xb, blk_expert)).reshape(P, D)
    y = jnp.zeros((T, D), h.dtype).at[rows_tok].add(yb * rows_gate[:, None])
    return y.reshape(B, S, D)


def setup_inputs(seed: int = 0) -> dict:
    key = jax.random.key(seed)
    ks = jax.random.split(key, 12)
    f32 = jnp.float32
    nrm = lambda k, shape, scale: jax.random.normal(k, shape, f32) * scale
    return {
        "x": nrm(ks[0], (BATCH, SEQ, D_MODEL), 1.0),
        "norm_mix_g": 1.0 + nrm(ks[1], (DEPTH, D_MODEL), 0.02),
        "w_in": nrm(ks[2], (DEPTH, D_MODEL, D_IN), D_MODEL ** -0.5),
        "w_out": nrm(ks[3], (DEPTH, D_CAT, D_MODEL), D_CAT ** -0.5),
        "norm_ffn_g": 1.0 + nrm(ks[4], (DEPTH, D_MODEL), 0.02),
        "w_router": nrm(ks[5], (DEPTH, D_MODEL, N_EXPERTS), D_MODEL ** -0.5),
        "b_router": nrm(ks[6], (DEPTH, N_EXPERTS), 0.01),
        "w_gate_up": nrm(ks[7], (DEPTH, N_EXPERTS, D_MODEL, 2 * D_EXPERT), D_MODEL ** -0.5),
        "b_gate_up": nrm(ks[8], (DEPTH, N_EXPERTS, 2 * D_EXPERT), 0.01),
        "w_down": nrm(ks[9], (DEPTH, N_EXPERTS, D_EXPERT, D_MODEL), D_EXPERT ** -0.5),
        "b_down": nrm(ks[10], (DEPTH, N_EXPERTS, D_MODEL), 0.01),
        "norm_final_g": 1.0 + nrm(ks[11], (D_MODEL,), 0.02),
    }


def reference(x, norm_mix_g, w_in, w_out, norm_ffn_g, w_router, b_router,
              w_gate_up, b_gate_up, w_down, b_down, norm_final_g):
    S = x.shape[1]
    pos = jnp.arange(S, dtype=jnp.float32)
    for l in range(DEPTH):
        x = x + hybrid_mixer(rms_norm(x, norm_mix_g[l]), w_in[l], w_out[l], pos)
        x = x + moe_ffn(rms_norm(x, norm_ffn_g[l]), w_router[l], b_router[l],
                        w_gate_up[l], b_gate_up[l], w_down[l], b_down[l])
    return rms_norm(x, norm_final_g)
```

```python
import functools

import jax
import jax.numpy as jnp
import numpy as np
from jax import lax
from jax.experimental import pallas as pl
from jax.experimental.pallas import tpu as pltpu

D_MODEL = 1024
N_HEADS = 4
D_HEAD = 128
RET_CHUNK = 128
ATT_BLOCK = 128
DILATED_CONFIGS = ((128, 1), (512, 4), (2048, 16))
ROPE_THETA = 10000.0
N_EXPERTS = 32
TOP_K = 4
D_EXPERT = D_MODEL
SWIGLU_LIMIT = 7.0
SWIGLU_ALPHA = 1.702
NORM_EPS = 1e-5
N_COL_BLOCKS = 28
RQ, RK, RV, RG, AQ, AK, AV = 0, 4, 8, 12, 16, 20, 24

LANES = 128
HALF = D_MODEL // 2
ROW_TILES = HALF // LANES
MOE_ROWS = 256
META_ROWS = 24
NEG_BIG = -1e30
VMEM_LIMIT = 48 * 1024 * 1024

BF16 = jnp.bfloat16
F32 = jnp.float32
U32 = jnp.uint32
I32 = jnp.int32


def _params(semantics):
    return pltpu.CompilerParams(dimension_semantics=semantics, vmem_limit_bytes=VMEM_LIMIT)


def _pack_bf16_pair(lo_f32, hi_f32):
    lo = lax.bitcast_convert_type(lo_f32.astype(BF16).astype(F32), U32)
    hi = lax.bitcast_convert_type(hi_f32.astype(BF16).astype(F32), U32)
    return (hi & jnp.uint32(0xFFFF0000)) | (lo >> 16)


def _unpack_bf16_pair(u):
    lo = lax.bitcast_convert_type(u << 16, F32)
    hi = lax.bitcast_convert_type(u & jnp.uint32(0xFFFF0000), F32)
    return lo, hi


def _store_packed_rows(ref, y):
    for c in range(ROW_TILES):
        ref[:, c, :] = _pack_bf16_pair(y[:, c * LANES:(c + 1) * LANES],
                                       y[:, HALF + c * LANES:HALF + (c + 1) * LANES])


def _load_packed_rows(ref):
    pairs = [_unpack_bf16_pair(ref[:, c, :]) for c in range(ROW_TILES)]
    return jnp.concatenate([p[0] for p in pairs] + [p[1] for p in pairs], axis=1)


def _inproj_body(x_ref, g_ref, w_ref, cos_ref, sin_ref, out_ref):
    x = x_ref[...]
    ms = jnp.mean(x * x, axis=-1, keepdims=True)
    h = (x * lax.rsqrt(ms + NORM_EPS) * g_ref[...]).astype(BF16)
    cos = cos_ref[...]
    sin = sin_ref[...]
    scale = D_HEAD ** -0.5
    group = N_HEADS * D_HEAD
    for grp in range(N_COL_BLOCKS // N_HEADS):
        p = jnp.dot(h, w_ref[:, grp * group:(grp + 1) * group], preferred_element_type=F32)
        first = grp * N_HEADS
        for hh in range(N_HEADS):
            t = p[:, hh * D_HEAD:(hh + 1) * D_HEAD]
            if first in (RQ, RK, AQ, AK):
                t = t * cos + pltpu.roll(t, D_HEAD // 2, 1) * sin
            if first in (RK, AQ):
                t = t * scale
            out_ref[first + hh] = t.astype(BF16)


def _in_proj(x2, g, w_in_bf, cos_t, sin_t, seq, tm):
    T = x2.shape[0]
    n_seq_tiles = seq // tm
    return pl.pallas_call(
        _inproj_body,
        grid=(T // tm,),
        in_specs=[
            pl.BlockSpec((tm, D_MODEL), lambda i: (i, 0)),
            pl.BlockSpec((1, D_MODEL), lambda i: (0, 0)),
            pl.BlockSpec(w_in_bf.shape, lambda i: (0, 0)),
            pl.BlockSpec((tm, D_HEAD), lambda i: (i % n_seq_tiles, 0)),
            pl.BlockSpec((tm, D_HEAD), lambda i: (i % n_seq_tiles, 0)),
        ],
        out_specs=pl.BlockSpec((N_COL_BLOCKS, tm, D_HEAD), lambda i: (0, i, 0)),
        out_shape=jax.ShapeDtypeStruct((N_COL_BLOCKS, T, D_HEAD), BF16),
        compiler_params=_params(("parallel",)),
        name="in_proj",
    )(x2, g, w_in_bf, cos_t, sin_t)


def _retention_body(q_ref, k_ref, v_ref, g_ref, decay_ref, zeta_ref, xi_ref, o_ref, state_ref, *, n_chunks):
    @pl.when(pl.program_id(2) == 0)
    def _():
        state_ref[...] = jnp.zeros_like(state_ref)

    decay = decay_ref[...]
    zeta = zeta_ref[...]
    xi = xi_ref[...]
    chunk_decay = xi[RET_CHUNK - 1:RET_CHUNK, :]
    st = state_ref[...]
    for c in range(n_chunks):
        rows = pl.ds(c * RET_CHUNK, RET_CHUNK)
        q = q_ref[rows, :]
        k = k_ref[rows, :]
        v = v_ref[rows, :]
        s = lax.dot_general(q, k, (((1,), (1,)), ((), ())), preferred_element_type=F32) * decay
        intra = jnp.dot(s.astype(BF16), v, preferred_element_type=F32)
        cross = jnp.dot(q, st.astype(BF16), preferred_element_type=F32) * xi
        o = intra + cross
        kz = (k.astype(F32) * zeta).astype(BF16)
        kv = lax.dot_general(kz, v, (((0,), (0,)), ((), ())), preferred_element_type=F32)
        st = st * chunk_decay + kv
        mu = jnp.mean(o, axis=-1, keepdims=True)
        oc = o - mu
        var = jnp.mean(oc * oc, axis=-1, keepdims=True)
        gate = g_ref[rows, :].astype(F32)
        o_ref[rows, :] = (gate * jax.nn.sigmoid(gate) * oc * lax.rsqrt(var + NORM_EPS)).astype(BF16)
    state_ref[...] = st


def _retention(p4, decay_t, zeta_t, xi_t, tc):
    _, B, S, _ = p4.shape

    def head_spec(first):
        return pl.BlockSpec((None, None, tc, D_HEAD), lambda b, h, n: (first + h, b, n, 0))

    table_spec = pl.BlockSpec((None, RET_CHUNK, D_HEAD), lambda b, h, n: (h, 0, 0))
    return pl.pallas_call(
        functools.partial(_retention_body, n_chunks=tc // RET_CHUNK),
        grid=(B, N_HEADS, S // tc),
        in_specs=[head_spec(RQ), head_spec(RK), head_spec(RV), head_spec(RG),
                  table_spec, table_spec, table_spec],
        out_specs=pl.BlockSpec((None, None, tc, D_HEAD), lambda b, h, n: (h, b, n, 0)),
        out_shape=jax.ShapeDtypeStruct((N_HEADS, B, S, D_HEAD), BF16),
        scratch_shapes=[pltpu.VMEM((D_HEAD, D_HEAD), F32)],
        compiler_params=_params(("parallel", "parallel", "arbitrary")),
        name="retention",
    )(p4, p4, p4, p4, decay_t, zeta_t, xi_t)


def _dilated_body(q_ref, k_ref, v_ref, kp_ref, vp_ref, o_ref, lse_ref, *, n_sub):
    row = lax.broadcasted_iota(I32, (ATT_BLOCK, ATT_BLOCK), 0)
    col = lax.broadcasted_iota(I32, (ATT_BLOCK, ATT_BLOCK), 1)
    cur_ok = col <= row
    has_prev = pl.program_id(3) > 0
    nt = (((1,), (1,)), ((), ()))
    for i in range(n_sub):
        rows = pl.ds(i * ATT_BLOCK, ATT_BLOCK)
        q = q_ref[rows, :]
        k_cur = k_ref[rows, :]
        v_cur = v_ref[rows, :]
        if i == 0:
            k_prev, v_prev = kp_ref[...], vp_ref[...]
            prev_ok = (col >= row) & has_prev
        else:
            prev_rows = pl.ds((i - 1) * ATT_BLOCK, ATT_BLOCK)
            k_prev, v_prev = k_ref[prev_rows, :], v_ref[prev_rows, :]
            prev_ok = col >= row
        s_prev = jnp.where(prev_ok, lax.dot_general(q, k_prev, nt, preferred_element_type=F32), NEG_BIG)
        s_cur = jnp.where(cur_ok, lax.dot_general(q, k_cur, nt, preferred_element_type=F32), NEG_BIG)
        m = jnp.maximum(jnp.max(s_prev, axis=-1, keepdims=True), jnp.max(s_cur, axis=-1, keepdims=True))
        p_prev = jnp.exp(s_prev - m)
        p_cur = jnp.exp(s_cur - m)
        den = jnp.sum(p_prev, axis=-1, keepdims=True) + jnp.sum(p_cur, axis=-1, keepdims=True)
        acc = (jnp.dot(p_prev.astype(BF16), v_prev, preferred_element_type=F32)
               + jnp.dot(p_cur.astype(BF16), v_cur, preferred_element_type=F32))
        o_ref[rows, :] = (acc / den).astype(BF16)
        lse_ref[rows, :] = jnp.broadcast_to(m + jnp.log(den), (ATT_BLOCK, D_HEAD))


def _dilated(p4, window, dilation, tq):
    _, B, S, _ = p4.shape
    assert window // dilation == ATT_BLOCK, "band masks assume a window of one block per subsequence"
    L = S // dilation
    assert L % tq == 0 and tq % ATT_BLOCK == 0
    n_sub = tq // ATT_BLOCK
    pv = p4.reshape(N_COL_BLOCKS, B, L, dilation * D_HEAD)

    def cur_spec(first):
        return pl.BlockSpec((None, None, tq, D_HEAD), lambda h, b, r, n: (first + h, b, n, r))

    def prev_spec(first):
        return pl.BlockSpec((None, None, ATT_BLOCK, D_HEAD),
                            lambda h, b, r, n: (first + h, b, jnp.maximum(n * n_sub - 1, 0), r))

    out_spec = pl.BlockSpec((None, None, tq, D_HEAD), lambda h, b, r, n: (h, b, n, r))
    o, lse = pl.pallas_call(
        functools.partial(_dilated_body, n_sub=n_sub),
        grid=(N_HEADS, B, dilation, L // tq),
        in_specs=[cur_spec(AQ), cur_spec(AK), cur_spec(AV), prev_spec(AK), prev_spec(AV)],
        out_specs=[out_spec, out_spec],
        out_shape=[jax.ShapeDtypeStruct((N_HEADS, B, L, dilation * D_HEAD), BF16),
                   jax.ShapeDtypeStruct((N_HEADS, B, L, dilation * D_HEAD), F32)],
        compiler_params=_params(("parallel", "parallel", "parallel", "arbitrary")),
        name=f"dilated_{dilation}",
    )(pv, pv, pv, pv, pv)
    T = B * S
    return o.reshape(N_HEADS, T, D_HEAD), lse.reshape(N_HEADS, T, D_HEAD)


def _outproj_body(x_ref, ret_ref, o1_ref, o4_ref, o16_ref, l1_ref, l4_ref, l16_ref, wout_ref, g2_ref,
                  wr_ref, br_ref, tri_ref,
                  x1_ref, h2p_ref, meta_ref, gcol_ref, counts_ref, carry_ref):
    tm = x_ref.shape[0]

    @pl.when(pl.program_id(0) == 0)
    def _():
        carry_ref[...] = jnp.zeros_like(carry_ref)

    pieces = [ret_ref[hh] for hh in range(N_HEADS)]
    for hh in range(N_HEADS):
        l1, l4, l16 = l1_ref[hh], l4_ref[hh], l16_ref[hh]
        m = jnp.maximum(jnp.maximum(l1, l4), l16)
        e1, e4, e16 = jnp.exp(l1 - m), jnp.exp(l4 - m), jnp.exp(l16 - m)
        mix = (e1 * o1_ref[hh].astype(F32) + e4 * o4_ref[hh].astype(F32) + e16 * o16_ref[hh].astype(F32))
        pieces.append((mix / (e1 + e4 + e16)).astype(BF16))
    cat = jnp.concatenate(pieces, axis=1)
    x1 = x_ref[...] + jnp.dot(cat, wout_ref[...], preferred_element_type=F32)
    x1_ref[...] = x1

    ms = jnp.mean(x1 * x1, axis=-1, keepdims=True)
    h2 = x1 * lax.rsqrt(ms + NORM_EPS) * g2_ref[...]
    _store_packed_rows(h2p_ref, h2)

    nt = (((1,), (1,)), ((), ()))
    h_hi = h2.astype(BF16)
    h_lo = (h2 - h_hi.astype(F32)).astype(BF16)
    wr = wr_ref[...]
    w_hi = wr.astype(BF16)
    w_lo = (wr - w_hi.astype(F32)).astype(BF16)
    logits = (lax.dot_general(w_hi, h_hi, nt, preferred_element_type=F32)
              + lax.dot_general(w_hi, h_lo, nt, preferred_element_type=F32)
              + lax.dot_general(w_lo, h_hi, nt, preferred_element_type=F32)) + br_ref[...]

    expert = lax.broadcasted_iota(I32, (N_EXPERTS, tm), 0)
    work = logits
    vals, ids, hots = [], [], []
    for _ in range(TOP_K):
        best = jnp.max(work, axis=0, keepdims=True)
        idx = jnp.min(jnp.where(work == best, expert, N_EXPERTS), axis=0, keepdims=True)
        hot = expert == idx
        work = jnp.where(hot, -jnp.inf, work)
        vals.append(best)
        ids.append(idx)
        hots.append(hot)
    exps = [jnp.exp(v - vals[0]) for v in vals]
    den = exps[0] + exps[1] + exps[2] + exps[3]
    gates = [e / den for e in exps]

    chosen = (hots[0] | hots[1] | hots[2] | hots[3]).astype(F32)
    before = jnp.dot(chosen.astype(BF16), tri_ref[...], preferred_element_type=F32) + carry_ref[...]
    ranks = [jnp.sum(jnp.where(hot, before, 0.0), axis=0, keepdims=True) for hot in hots]
    carry_ref[...] = carry_ref[...] + jnp.sum(chosen, axis=1, keepdims=True)
    counts_ref[...] = carry_ref[...]

    sub = lax.broadcasted_iota(I32, (8, tm), 0)

    def rows8(rows):
        out = jnp.zeros((8, tm), F32)
        for kk, r in enumerate(rows):
            out = jnp.where(sub == kk, jnp.broadcast_to(r, (8, tm)), out)
        return out

    gate8 = rows8(gates)
    meta_ref[0:8, :] = rows8([i.astype(F32) for i in ids])
    meta_ref[8:16, :] = rows8(ranks)
    meta_ref[16:24, :] = gate8
    gate128 = jnp.concatenate([gate8] + [jnp.zeros((8, tm), F32)] * (LANES // 8 - 1), axis=0)
    gcol_ref[...] = gate128.T


def _out_proj(x2, ret, branches, w_out_bf, g2, w_rt, b_r, tri, tm):
    T = x2.shape[0]
    head_spec = pl.BlockSpec((N_HEADS, tm, D_HEAD), lambda i: (0, i, 0))

    def full(a):
        return pl.BlockSpec(a.shape, lambda i: (0,) * a.ndim)

    (o1, l1), (o4, l4), (o16, l16) = branches
    return pl.pallas_call(
        _outproj_body,
        grid=(T // tm,),
        in_specs=[pl.BlockSpec((tm, D_MODEL), lambda i: (i, 0)),
                  head_spec, head_spec, head_spec, head_spec, head_spec, head_spec, head_spec,
                  full(w_out_bf), full(g2), full(w_rt), full(b_r), full(tri)],
        out_specs=[pl.BlockSpec((tm, D_MODEL), lambda i: (i, 0)),
                   pl.BlockSpec((tm, ROW_TILES, LANES), lambda i: (i, 0, 0)),
                   pl.BlockSpec((META_ROWS, tm), lambda i: (0, i)),
                   pl.BlockSpec((tm, LANES), lambda i: (i, 0)),
                   pl.BlockSpec((N_EXPERTS, 1), lambda i: (0, 0))],
        out_shape=[jax.ShapeDtypeStruct((T, D_MODEL), F32),
                   jax.ShapeDtypeStruct((T, ROW_TILES, LANES), U32),
                   jax.ShapeDtypeStruct((META_ROWS, T), F32),
                   jax.ShapeDtypeStruct((T, LANES), F32),
                   jax.ShapeDtypeStruct((N_EXPERTS, 1), F32)],
        scratch_shapes=[pltpu.VMEM((N_EXPERTS, 1), F32)],
        compiler_params=_params(("arbitrary",)),
        name="out_proj_router",
    )(x2, ret, o1, o4, o16, l1, l4, l16, w_out_bf, g2, w_rt, b_r, tri)


def _dest_body(pstart_ref, meta_ref, dest_ref):
    e = meta_ref[0:8, :].astype(I32)
    acc = meta_ref[8:16, :].astype(I32)
    for ex in range(N_EXPERTS):
        acc = acc + jnp.where(e == ex, pstart_ref[ex], 0)
    dest_ref[...] = acc


def _dest_rows(pstart, meta, te):
    T = meta.shape[1]
    return pl.pallas_call(
        _dest_body,
        grid_spec=pltpu.PrefetchScalarGridSpec(
            num_scalar_prefetch=1,
            grid=(T // te,),
            in_specs=[pl.BlockSpec((META_ROWS, te), lambda i, ps: (0, i))],
            out_specs=pl.BlockSpec((8, te), lambda i, ps: (0, i)),
        ),
        out_shape=jax.ShapeDtypeStruct((8, T), I32),
        compiler_params=_params(("parallel",)),
        name="dest_rows",
    )(pstart, meta)


def _dispatch_body(pend_ref, padded_ref, dest_ref, h_ref, xg_ref, zero_ref, sem, zsem):
    tf = h_ref.shape[0]

    def zero_copy(e):
        start = pl.multiple_of(pend_ref[e] - MOE_ROWS, MOE_ROWS)
        return pltpu.make_async_copy(zero_ref, xg_ref.at[pl.ds(start, MOE_ROWS)], zsem)

    @pl.when(pl.program_id(0) == 0)
    def _():
        zero_ref[...] = jnp.zeros_like(zero_ref)
        for e in range(N_EXPERTS):
            @pl.when(padded_ref[e] > 0)
            def _():
                zero_copy(e).start()
        for e in range(N_EXPERTS):
            @pl.when(padded_ref[e] > 0)
            def _():
                zero_copy(e).wait()

    def row_copy(r, kk):
        return pltpu.make_async_copy(h_ref.at[r], xg_ref.at[dest_ref[kk, r]], sem)

    def issue(r, carry):
        for kk in range(TOP_K):
            row_copy(r, kk).start()
        return carry

    def drain(r, carry):
        for kk in range(TOP_K):
            row_copy(r, kk).wait()
        return carry

    lax.fori_loop(0, tf, issue, 0)
    lax.fori_loop(0, tf, drain, 0)


def _dispatch(pend, padded, dest, h2p, n_rows, tf):
    T = h2p.shape[0]
    return pl.pallas_call(
        _dispatch_body,
        grid_spec=pltpu.PrefetchScalarGridSpec(
            num_scalar_prefetch=2,
            grid=(T // tf,),
            in_specs=[pl.BlockSpec((8, tf), lambda i, a, b: (0, i), memory_space=pltpu.SMEM),
                      pl.BlockSpec((tf, ROW_TILES, LANES), lambda i, a, b: (i, 0, 0))],
            out_specs=pl.BlockSpec(memory_space=pl.ANY),
            scratch_shapes=[pltpu.VMEM((MOE_ROWS, ROW_TILES, LANES), U32),
                            pltpu.SemaphoreType.DMA(()), pltpu.SemaphoreType.DMA(())],
        ),
        out_shape=jax.ShapeDtypeStruct((n_rows, ROW_TILES, LANES), U32),
        compiler_params=_params(("arbitrary",)),
        name="dispatch",
    )(pend, padded, dest, h2p)


def _experts_body(be_ref, nu_ref, x_ref, wgu_ref, bgu_ref, wdn_ref, bdn_ref, o_ref):
    @pl.when(pl.program_id(0) < nu_ref[0])
    def _():
        xb = _load_packed_rows(x_ref).astype(BF16)
        width = 256
        acc = jnp.zeros((x_ref.shape[0], D_MODEL), F32)
        for c in range(D_EXPERT // width):
            glu_cols = pl.ds(c * width, width)
            lin_cols = pl.ds(D_EXPERT + c * width, width)
            glu = jnp.dot(xb, wgu_ref[:, glu_cols], preferred_element_type=F32) + bgu_ref[:, glu_cols]
            lin = jnp.dot(xb, wgu_ref[:, lin_cols], preferred_element_type=F32) + bgu_ref[:, lin_cols]
            glu = jnp.minimum(glu, SWIGLU_LIMIT)
            lin = jnp.clip(lin, -SWIGLU_LIMIT, SWIGLU_LIMIT)
            act = glu * jax.nn.sigmoid(SWIGLU_ALPHA * glu) * (lin + 1.0)
            acc = acc + jnp.dot(act.astype(BF16), wdn_ref[glu_cols, :], preferred_element_type=F32)
        _store_packed_rows(o_ref, acc + bdn_ref[...])


def _experts(blk_expert, n_used, xg, w_gu_bf, b_gu, w_dn_bf, b_dn):
    n_rows = xg.shape[0]
    n_blk = n_rows // MOE_ROWS

    def live(j, nu):
        return jnp.minimum(j, nu[0] - 1)

    return pl.pallas_call(
        _experts_body,
        grid_spec=pltpu.PrefetchScalarGridSpec(
            num_scalar_prefetch=2,
            grid=(n_blk,),
            in_specs=[
                pl.BlockSpec((MOE_ROWS, ROW_TILES, LANES), lambda j, be, nu: (live(j, nu), 0, 0)),
                pl.BlockSpec((None, D_MODEL, 2 * D_EXPERT), lambda j, be, nu: (be[live(j, nu)], 0, 0)),
                pl.BlockSpec((None, 1, 2 * D_EXPERT), lambda j, be, nu: (be[live(j, nu)], 0, 0)),
                pl.BlockSpec((None, D_EXPERT, D_MODEL), lambda j, be, nu: (be[live(j, nu)], 0, 0)),
                pl.BlockSpec((None, 1, D_MODEL), lambda j, be, nu: (be[live(j, nu)], 0, 0)),
            ],
            out_specs=pl.BlockSpec((MOE_ROWS, ROW_TILES, LANES), lambda j, be, nu: (live(j, nu), 0, 0)),
        ),
        out_shape=jax.ShapeDtypeStruct((n_rows, ROW_TILES, LANES), U32),
        compiler_params=_params(("arbitrary",)),
        name="experts",
    )(blk_expert, n_used, xg, w_gu_bf, b_gu, w_dn_bf, b_dn)


def _combine_body(dest_ref, x1_ref, gcol_ref, gf_ref, yb_ref, out_ref, rows_ref, sem):
    th = x1_ref.shape[0]

    def row_copy(r, kk):
        return pltpu.make_async_copy(yb_ref.at[dest_ref[kk, r]], rows_ref.at[kk, r], sem)

    def issue(r, carry):
        for kk in range(TOP_K):
            row_copy(r, kk).start()
        return carry

    def drain(r, carry):
        for kk in range(TOP_K):
            row_copy(r, kk).wait()
        return carry

    lax.fori_loop(0, th, issue, 0)
    lax.fori_loop(0, th, drain, 0)

    gates = gcol_ref[...]
    x2 = x1_ref[...]
    for kk in range(TOP_K):
        x2 = x2 + gates[:, kk:kk + 1] * _load_packed_rows(rows_ref.at[kk])
    ms = jnp.mean(x2 * x2, axis=-1, keepdims=True)
    out_ref[...] = x2 * lax.rsqrt(ms + NORM_EPS) * gf_ref[...]


def _combine(dest, x1, gcol, g_final, yb, th):
    T = x1.shape[0]
    return pl.pallas_call(
        _combine_body,
        grid=(T // th,),
        in_specs=[pl.BlockSpec((8, th), lambda i: (0, i), memory_space=pltpu.SMEM),
                  pl.BlockSpec((th, D_MODEL), lambda i: (i, 0)),
                  pl.BlockSpec((th, LANES), lambda i: (i, 0)),
                  pl.BlockSpec((1, D_MODEL), lambda i: (0, 0)),
                  pl.BlockSpec(memory_space=pl.ANY)],
        out_specs=pl.BlockSpec((th, D_MODEL), lambda i: (i, 0)),
        out_shape=jax.ShapeDtypeStruct((T, D_MODEL), F32),
        scratch_shapes=[pltpu.VMEM((TOP_K, th, ROW_TILES, LANES), U32), pltpu.SemaphoreType.DMA(())],
        compiler_params=_params(("arbitrary",)),
        name="combine_final",
    )(dest, x1, gcol, g_final, yb)


def _rotary_tables(seq):
    inv = ROPE_THETA ** (-jnp.arange(0, D_HEAD, 2, dtype=F32) / D_HEAD)
    ang = jnp.arange(seq, dtype=F32)[:, None] * inv[None, :]
    cos, sin = jnp.cos(ang), jnp.sin(ang)
    return jnp.concatenate([cos, cos], axis=1), jnp.concatenate([-sin, sin], axis=1)


def _retention_tables():
    C = RET_CHUNK
    log_g = jnp.log1p(-(2.0 ** (-5.0 - jnp.arange(N_HEADS, dtype=F32))))
    idx = jnp.arange(C, dtype=F32)
    diff = idx[:, None] - idx[None, :]
    decay = jnp.where(diff[None] >= 0, jnp.exp(jnp.maximum(diff, 0.0)[None] * log_g[:, None, None]), 0.0)
    zeta = jnp.exp((C - 1 - idx)[None, :] * log_g[:, None])
    xi = jnp.exp((idx + 1)[None, :] * log_g[:, None])
    wide = (N_HEADS, C, D_HEAD)
    return decay, jnp.broadcast_to(zeta[:, :, None], wide), jnp.broadcast_to(xi[:, :, None], wide)


def _layer(x2, batch, seq, norm_mix_g, w_in, w_out, norm_ffn_g, w_router, b_router,
           w_gate_up, b_gate_up, w_down, b_down, out_gain):
    T = x2.shape[0]
    tm = min(512, seq)
    cos_t, sin_t = _rotary_tables(seq)
    proj = _in_proj(x2, norm_mix_g.reshape(1, D_MODEL), w_in.astype(BF16), cos_t, sin_t, seq, tm)
    p4 = proj.reshape(N_COL_BLOCKS, batch, seq, D_HEAD)

    ret = _retention(p4, *_retention_tables(), tc=min(1024, seq)).reshape(N_HEADS, T, D_HEAD)
    branches = [_dilated(p4, w, d, tq=min(512, seq // d)) for w, d in DILATED_CONFIGS]

    tri = (jnp.arange(tm)[:, None] < jnp.arange(tm)[None, :]).astype(BF16)
    x1, h2p, meta, gcol, counts = _out_proj(
        x2, ret, branches, w_out.astype(BF16), norm_ffn_g.reshape(1, D_MODEL),
        w_router.T, b_router.reshape(N_EXPERTS, 1), tri, tm)

    cnt = counts[:, 0].astype(I32)
    padded = (cnt + MOE_ROWS - 1) // MOE_ROWS * MOE_ROWS
    pend = jnp.cumsum(padded).astype(I32)
    pstart = pend - padded
    n_rows = T * TOP_K + N_EXPERTS * MOE_ROWS
    n_blk = n_rows // MOE_ROWS
    blk_expert = jnp.minimum(
        jnp.searchsorted(pend, jnp.arange(n_blk, dtype=I32) * MOE_ROWS, side="right"),
        N_EXPERTS - 1).astype(I32)
    n_used = (pend[-1:] // MOE_ROWS).astype(I32)

    dest = _dest_rows(pstart, meta, te=min(4096, T))
    xg = _dispatch(pend, padded, dest, h2p, n_rows, tf=min(256, T))
    yb = _experts(blk_expert, n_used, xg, w_gate_up.astype(BF16),
                  b_gate_up.reshape(N_EXPERTS, 1, 2 * D_EXPERT), w_down.astype(BF16),
                  b_down.reshape(N_EXPERTS, 1, D_MODEL))
    return _combine(dest, x1, gcol, out_gain, yb, th=min(256, T))


def kernel(x, norm_mix_g, w_in, w_out, norm_ffn_g, w_router, b_router, w_gate_up, b_gate_up, w_down,
           b_down, norm_final_g):
    batch, seq, _ = x.shape
    depth = w_in.shape[0]
    assert depth == 1, "the final RMSNorm is fused into the last layer's combine kernel"
    out = _layer(x.reshape(batch * seq, D_MODEL), batch, seq, norm_mix_g[0], w_in[0], w_out[0],
                 norm_ffn_g[0], w_router[0], b_router[0], w_gate_up[0], b_gate_up[0], w_down[0],
                 b_down[0], norm_final_g.reshape(1, D_MODEL))
    return out.reshape(batch, seq, D_MODEL)
```

```python
import functools

import jax
import jax.numpy as jnp
import numpy as np
from jax import lax
from jax.experimental import pallas as pl
from jax.experimental.pallas import tpu as pltpu

D_MODEL = 1024
N_HEADS = 4
D_HEAD = 128
RET_CHUNK = 128
ATT_BLOCK = 128
DILATED_CONFIGS = ((128, 1), (512, 4), (2048, 16))
ROPE_THETA = 10000.0
N_EXPERTS = 32
TOP_K = 4
D_EXPERT = D_MODEL
SWIGLU_LIMIT = 7.0
SWIGLU_ALPHA = 1.702
NORM_EPS = 1e-5
N_COL_BLOCKS = 28
RQ, RK, RV, RG, AQ, AK, AV = 0, 4, 8, 12, 16, 20, 24

LANES = 128
HALF = D_MODEL // 2
ROW_TILES = HALF // LANES
SUPER = ATT_BLOCK * 16
TILE_GROUP = 4
MOE_ROWS = 256
META_ROWS = 24
NEG_BIG = -1e30
VMEM_LIMIT = 48 * 1024 * 1024

BF16 = jnp.bfloat16
F32 = jnp.float32
U32 = jnp.uint32
I32 = jnp.int32


def _params(semantics):
    return pltpu.CompilerParams(dimension_semantics=semantics, vmem_limit_bytes=VMEM_LIMIT)


def _pack_bf16_pair(lo_f32, hi_f32):
    lo = lax.bitcast_convert_type(lo_f32.astype(BF16).astype(F32), U32)
    hi = lax.bitcast_convert_type(hi_f32.astype(BF16).astype(F32), U32)
    return (hi & jnp.uint32(0xFFFF0000)) | (lo >> 16)


def _unpack_bf16_pair(u):
    lo = lax.bitcast_convert_type(u << 16, F32)
    hi = lax.bitcast_convert_type(u & jnp.uint32(0xFFFF0000), F32)
    return lo, hi


def _store_packed_rows(ref, y):
    for c in range(ROW_TILES):
        ref[:, c, :] = _pack_bf16_pair(y[:, c * LANES:(c + 1) * LANES],
                                       y[:, HALF + c * LANES:HALF + (c + 1) * LANES])


def _load_packed_rows(ref):
    pairs = [_unpack_bf16_pair(ref[:, c, :]) for c in range(ROW_TILES)]
    return jnp.concatenate([p[0] for p in pairs] + [p[1] for p in pairs], axis=1)


def _inproj_body(x_ref, g_ref, w_ref, cos_ref, sin_ref, ret_ref, att_ref):
    x = x_ref[...]
    ms = jnp.mean(x * x, axis=-1, keepdims=True)
    h = (x * lax.rsqrt(ms + NORM_EPS) * g_ref[...]).astype(BF16)
    cos = cos_ref[...]
    sin = sin_ref[...]
    scale = D_HEAD ** -0.5
    group = N_HEADS * D_HEAD
    for grp in range(N_COL_BLOCKS // N_HEADS):
        p = jnp.dot(h, w_ref[:, grp * group:(grp + 1) * group], preferred_element_type=F32)
        first = grp * N_HEADS
        for hh in range(N_HEADS):
            t = p[:, hh * D_HEAD:(hh + 1) * D_HEAD]
            if first in (RQ, RK, AQ, AK):
                t = t * cos + pltpu.roll(t, D_HEAD // 2, 1) * sin
            if first in (RK, AQ):
                t = t * scale
            if first < AQ:
                ret_ref[first + hh] = t.astype(BF16)
            else:
                att_ref[first - AQ + hh] = t


def _in_proj(x2, g, w_in_bf, cos_t, sin_t, seq, tm):
    T = x2.shape[0]
    n_seq_tiles = seq // tm
    return pl.pallas_call(
        _inproj_body,
        grid=(T // tm,),
        in_specs=[
            pl.BlockSpec((tm, D_MODEL), lambda i: (i, 0)),
            pl.BlockSpec((1, D_MODEL), lambda i: (0, 0)),
            pl.BlockSpec(w_in_bf.shape, lambda i: (0, 0)),
            pl.BlockSpec((tm, D_HEAD), lambda i: (i % n_seq_tiles, 0)),
            pl.BlockSpec((tm, D_HEAD), lambda i: (i % n_seq_tiles, 0)),
        ],
        out_specs=[pl.BlockSpec((AQ, tm, D_HEAD), lambda i: (0, i, 0)),
                   pl.BlockSpec((N_COL_BLOCKS - AQ, tm, D_HEAD), lambda i: (0, i, 0))],
        out_shape=[jax.ShapeDtypeStruct((AQ, T, D_HEAD), BF16),
                   jax.ShapeDtypeStruct((N_COL_BLOCKS - AQ, T, D_HEAD), F32)],
        compiler_params=_params(("parallel",)),
        name="in_proj",
    )(x2, g, w_in_bf, cos_t, sin_t)


def _retention_body(q_ref, k_ref, v_ref, g_ref, decay_ref, zeta_ref, xi_ref, o_ref, state_ref, *, n_chunks):
    @pl.when(pl.program_id(2) == 0)
    def _():
        state_ref[...] = jnp.zeros_like(state_ref)

    decay = decay_ref[...]
    zeta = zeta_ref[...]
    xi = xi_ref[...]
    chunk_decay = xi[RET_CHUNK - 1:RET_CHUNK, :]
    st = state_ref[...]
    for c in range(n_chunks):
        rows = pl.ds(c * RET_CHUNK, RET_CHUNK)
        q = q_ref[rows, :]
        k = k_ref[rows, :]
        v = v_ref[rows, :]
        s = lax.dot_general(q, k, (((1,), (1,)), ((), ())), preferred_element_type=F32) * decay
        intra = jnp.dot(s.astype(BF16), v, preferred_element_type=F32)
        cross = jnp.dot(q, st.astype(BF16), preferred_element_type=F32) * xi
        o = intra + cross
        kz = (k.astype(F32) * zeta).astype(BF16)
        kv = lax.dot_general(kz, v, (((0,), (0,)), ((), ())), preferred_element_type=F32)
        st = st * chunk_decay + kv
        mu = jnp.mean(o, axis=-1, keepdims=True)
        oc = o - mu
        var = jnp.mean(oc * oc, axis=-1, keepdims=True)
        gate = g_ref[rows, :].astype(F32)
        o_ref[rows, :] = (gate * jax.nn.sigmoid(gate) * oc * lax.rsqrt(var + NORM_EPS)).astype(BF16)
    state_ref[...] = st


def _retention(p4, decay_t, zeta_t, xi_t, tc):
    _, B, S, _ = p4.shape

    def head_spec(first):
        return pl.BlockSpec((None, None, tc, D_HEAD), lambda b, h, n: (first + h, b, n, 0))

    table_spec = pl.BlockSpec((None, RET_CHUNK, D_HEAD), lambda b, h, n: (h, 0, 0))
    return pl.pallas_call(
        functools.partial(_retention_body, n_chunks=tc // RET_CHUNK),
        grid=(B, N_HEADS, S // tc),
        in_specs=[head_spec(RQ), head_spec(RK), head_spec(RV), head_spec(RG),
                  table_spec, table_spec, table_spec],
        out_specs=pl.BlockSpec((None, None, tc, D_HEAD), lambda b, h, n: (h, b, n, 0)),
        out_shape=jax.ShapeDtypeStruct((N_HEADS, B, S, D_HEAD), BF16),
        scratch_shapes=[pltpu.VMEM((D_HEAD, D_HEAD), F32)],
        compiler_params=_params(("parallel", "parallel", "arbitrary")),
        name="retention",
    )(p4, p4, p4, p4, decay_t, zeta_t, xi_t)


def _rows(ref, start, size, stride):
    if stride == 1:
        return ref[pl.ds(start, size), :]
    return ref[pl.ds(start, size, stride=stride), :]


def _attention_body(q_ref, kc_ref, vc_ref, kp_ref, vp_ref, out_ref, k_buf, v_buf, o4_ref, l4_ref, o16_ref, l16_ref):
    has_past = pl.program_id(2) > 0
    k_buf[0:SUPER, :] = kp_ref[...]
    k_buf[SUPER:2 * SUPER, :] = kc_ref[...]
    v_buf[0:SUPER, :] = vp_ref[...]
    v_buf[SUPER:2 * SUPER, :] = vc_ref[...]

    row = lax.broadcasted_iota(I32, (ATT_BLOCK, 2 * ATT_BLOCK), 0)
    col = lax.broadcasted_iota(I32, (ATT_BLOCK, 2 * ATT_BLOCK), 1)
    dist = row + ATT_BLOCK - col
    band = (dist >= 0) & (dist <= ATT_BLOCK)
    nt = (((1,), (1,)), ((), ()))

    def emit_scratch(o_ref, l_ref, d):
        def emit(start, o, lse):
            o_ref[pl.ds(start, ATT_BLOCK, stride=d), :] = o
            l_ref[pl.ds(start, ATT_BLOCK, stride=d), :] = jnp.broadcast_to(lse, (ATT_BLOCK, D_HEAD))
        return emit

    def emit_mixed(start, o1, l1):
        start = pl.multiple_of(start, ATT_BLOCK)
        rows = pl.ds(start, ATT_BLOCK)
        l4, l16 = l4_ref[rows, :], l16_ref[rows, :]
        m = jnp.maximum(jnp.maximum(l1, l4), l16)
        e1, e4, e16 = jnp.exp(l1 - m), jnp.exp(l4 - m), jnp.exp(l16 - m)
        mix = (e1 * o1 + e4 * o4_ref[rows, :] + e16 * o16_ref[rows, :]) / (e1 + e4 + e16)
        out_ref[rows, :] = mix.astype(BF16)

    def run(d, emit):
        span = ATT_BLOCK * d
        n_tiles = SUPER // ATT_BLOCK
        shift = d.bit_length() - 1

        def group(g, carry):
            starts, scores, values = [], [], []
            for j in range(TILE_GROUP):
                t = g * TILE_GROUP + j
                block, sub = t >> shift, t & (d - 1)
                start = block * span + sub
                q = _rows(q_ref, start, ATT_BLOCK, d).astype(BF16)
                k = _rows(k_buf, SUPER - span + start, 2 * ATT_BLOCK, d).astype(BF16)
                v = _rows(v_buf, SUPER - span + start, 2 * ATT_BLOCK, d).astype(BF16)
                mask = band & ((col >= ATT_BLOCK) | has_past | (block > 0))
                s = lax.dot_general(q, k, nt, preferred_element_type=F32)
                starts.append(start)
                scores.append(jnp.where(mask, s, NEG_BIG))
                values.append(v)
            probs, dens, lses = [], [], []
            for s in scores:
                m = jnp.max(s, axis=-1, keepdims=True)
                p = jnp.exp(s - m)
                den = jnp.sum(p, axis=-1, keepdims=True)
                probs.append(p.astype(BF16))
                dens.append(den)
                lses.append(m + jnp.log(den))
            for start, p, v, den, lse in zip(starts, probs, values, dens, lses):
                emit(start, jnp.dot(p, v, preferred_element_type=F32) / den, lse)
            return carry

        lax.fori_loop(0, n_tiles // TILE_GROUP, group, 0)

    run(16, emit_scratch(o16_ref, l16_ref, 16))
    run(4, emit_scratch(o4_ref, l4_ref, 4))
    run(1, emit_mixed)


def _attention(pa4):
    _, B, S, _ = pa4.shape
    assert S % SUPER == 0
    for window, dilation in DILATED_CONFIGS:
        assert window // dilation == ATT_BLOCK, "band mask assumes one block of past per subsequence"
    assert tuple(d for _, d in DILATED_CONFIGS) == (1, 4, 16)
    q0, k0, v0 = 0, N_HEADS, 2 * N_HEADS

    def cur_spec(first):
        return pl.BlockSpec((None, None, SUPER, D_HEAD), lambda b, h, n: (first + h, b, n, 0))

    def prev_spec(first):
        return pl.BlockSpec((None, None, SUPER, D_HEAD), lambda b, h, n: (first + h, b, jnp.maximum(n - 1, 0), 0))

    return pl.pallas_call(
        _attention_body,
        grid=(B, N_HEADS, S // SUPER),
        in_specs=[cur_spec(q0), cur_spec(k0), cur_spec(v0), prev_spec(k0), prev_spec(v0)],
        out_specs=pl.BlockSpec((None, None, SUPER, D_HEAD), lambda b, h, n: (h, b, n, 0)),
        out_shape=jax.ShapeDtypeStruct((N_HEADS, B, S, D_HEAD), BF16),
        scratch_shapes=[pltpu.VMEM((2 * SUPER, D_HEAD), F32)] * 2 + [pltpu.VMEM((SUPER, D_HEAD), F32)] * 4,
        compiler_params=_params(("parallel", "parallel", "arbitrary")),
        name="dilated_attention",
    )(pa4, pa4, pa4, pa4, pa4)


def _outproj_body(x_ref, ret_ref, att_ref, wout_ref, g2_ref, wr_ref, br_ref, tri_ref,
                  x1_ref, h2p_ref, meta_ref, gcol_ref, counts_ref, carry_ref):
    tm = x_ref.shape[0]

    @pl.when(pl.program_id(0) == 0)
    def _():
        carry_ref[...] = jnp.zeros_like(carry_ref)

    cat = jnp.concatenate([ret_ref[hh] for hh in range(N_HEADS)] + [att_ref[hh] for hh in range(N_HEADS)], axis=1)
    x1 = x_ref[...] + jnp.dot(cat, wout_ref[...], preferred_element_type=F32)
    x1_ref[...] = x1

    ms = jnp.mean(x1 * x1, axis=-1, keepdims=True)
    h2 = x1 * lax.rsqrt(ms + NORM_EPS) * g2_ref[...]
    _store_packed_rows(h2p_ref, h2)

    nt = (((1,), (1,)), ((), ()))
    h_hi = h2.astype(BF16)
    h_lo = (h2 - h_hi.astype(F32)).astype(BF16)
    wr = wr_ref[...]
    w_hi = wr.astype(BF16)
    w_lo = (wr - w_hi.astype(F32)).astype(BF16)
    logits = (lax.dot_general(w_hi, h_hi, nt, preferred_element_type=F32)
              + lax.dot_general(w_hi, h_lo, nt, preferred_element_type=F32)
              + lax.dot_general(w_lo, h_hi, nt, preferred_element_type=F32)) + br_ref[...]

    expert = lax.broadcasted_iota(I32, (N_EXPERTS, tm), 0)
    work = logits
    vals, ids, hots = [], [], []
    for _ in range(TOP_K):
        best = jnp.max(work, axis=0, keepdims=True)
        idx = jnp.min(jnp.where(work == best, expert, N_EXPERTS), axis=0, keepdims=True)
        hot = expert == idx
        work = jnp.where(hot, -jnp.inf, work)
        vals.append(best)
        ids.append(idx)
        hots.append(hot)
    exps = [jnp.exp(v - vals[0]) for v in vals]
    den = exps[0] + exps[1] + exps[2] + exps[3]
    gates = [e / den for e in exps]

    chosen = (hots[0] | hots[1] | hots[2] | hots[3]).astype(F32)
    before = jnp.dot(chosen.astype(BF16), tri_ref[...], preferred_element_type=F32) + carry_ref[...]
    ranks = [jnp.sum(jnp.where(hot, before, 0.0), axis=0, keepdims=True) for hot in hots]
    carry_ref[...] = carry_ref[...] + jnp.sum(chosen, axis=1, keepdims=True)
    counts_ref[...] = carry_ref[...]

    sub = lax.broadcasted_iota(I32, (8, tm), 0)

    def rows8(rows):
        out = jnp.zeros((8, tm), F32)
        for kk, r in enumerate(rows):
            out = jnp.where(sub == kk, jnp.broadcast_to(r, (8, tm)), out)
        return out

    gate8 = rows8(gates)
    meta_ref[0:8, :] = rows8([i.astype(F32) for i in ids])
    meta_ref[8:16, :] = rows8(ranks)
    meta_ref[16:24, :] = gate8
    gate128 = jnp.concatenate([gate8] + [jnp.zeros((8, tm), F32)] * (LANES // 8 - 1), axis=0)
    gcol_ref[...] = gate128.T


def _out_proj(x2, ret, att, w_out_bf, g2, w_rt, b_r, tri, tm):
    T = x2.shape[0]
    head_spec = pl.BlockSpec((N_HEADS, tm, D_HEAD), lambda i: (0, i, 0))

    def full(a):
        return pl.BlockSpec(a.shape, lambda i: (0,) * a.ndim)

    return pl.pallas_call(
        _outproj_body,
        grid=(T // tm,),
        in_specs=[pl.BlockSpec((tm, D_MODEL), lambda i: (i, 0)),
                  head_spec, head_spec,
                  full(w_out_bf), full(g2), full(w_rt), full(b_r), full(tri)],
        out_specs=[pl.BlockSpec((tm, D_MODEL), lambda i: (i, 0)),
                   pl.BlockSpec((tm, ROW_TILES, LANES), lambda i: (i, 0, 0)),
                   pl.BlockSpec((META_ROWS, tm), lambda i: (0, i)),
                   pl.BlockSpec((tm, LANES), lambda i: (i, 0)),
                   pl.BlockSpec((N_EXPERTS, 1), lambda i: (0, 0))],
        out_shape=[jax.ShapeDtypeStruct((T, D_MODEL), F32),
                   jax.ShapeDtypeStruct((T, ROW_TILES, LANES), U32),
                   jax.ShapeDtypeStruct((META_ROWS, T), F32),
                   jax.ShapeDtypeStruct((T, LANES), F32),
                   jax.ShapeDtypeStruct((N_EXPERTS, 1), F32)],
        scratch_shapes=[pltpu.VMEM((N_EXPERTS, 1), F32)],
        compiler_params=_params(("arbitrary",)),
        name="out_proj_router",
    )(x2, ret, att, w_out_bf, g2, w_rt, b_r, tri)


def _dest_body(pstart_ref, meta_ref, dest_ref):
    e = meta_ref[0:8, :].astype(I32)
    acc = meta_ref[8:16, :].astype(I32)
    for ex in range(N_EXPERTS):
        acc = acc + jnp.where(e == ex, pstart_ref[ex], 0)
    dest_ref[...] = acc


def _dest_rows(pstart, meta, te):
    T = meta.shape[1]
    return pl.pallas_call(
        _dest_body,
        grid_spec=pltpu.PrefetchScalarGridSpec(
            num_scalar_prefetch=1,
            grid=(T // te,),
            in_specs=[pl.BlockSpec((META_ROWS, te), lambda i, ps: (0, i))],
            out_specs=pl.BlockSpec((8, te), lambda i, ps: (0, i)),
        ),
        out_shape=jax.ShapeDtypeStruct((8, T), I32),
        compiler_params=_params(("parallel",)),
        name="dest_rows",
    )(pstart, meta)


def _dispatch_body(pend_ref, padded_ref, dest_ref, h_ref, xg_ref, zero_ref, sem, zsem):
    tf = h_ref.shape[0]

    def zero_copy(e):
        start = pl.multiple_of(pend_ref[e] - MOE_ROWS, MOE_ROWS)
        return pltpu.make_async_copy(zero_ref, xg_ref.at[pl.ds(start, MOE_ROWS)], zsem)

    @pl.when(pl.program_id(0) == 0)
    def _():
        zero_ref[...] = jnp.zeros_like(zero_ref)
        for e in range(N_EXPERTS):
            @pl.when(padded_ref[e] > 0)
            def _():
                zero_copy(e).start()
        for e in range(N_EXPERTS):
            @pl.when(padded_ref[e] > 0)
            def _():
                zero_copy(e).wait()

    def row_copy(r, kk):
        return pltpu.make_async_copy(h_ref.at[r], xg_ref.at[dest_ref[kk, r]], sem)

    def issue(r, carry):
        for kk in range(TOP_K):
            row_copy(r, kk).start()
        return carry

    def drain(r, carry):
        for kk in range(TOP_K):
            row_copy(r, kk).wait()
        return carry

    lax.fori_loop(0, tf, issue, 0)
    lax.fori_loop(0, tf, drain, 0)


def _dispatch(pend, padded, dest, h2p, n_rows, tf):
    T = h2p.shape[0]
    return pl.pallas_call(
        _dispatch_body,
        grid_spec=pltpu.PrefetchScalarGridSpec(
            num_scalar_prefetch=2,
            grid=(T // tf,),
            in_specs=[pl.BlockSpec((8, tf), lambda i, a, b: (0, i), memory_space=pltpu.SMEM),
                      pl.BlockSpec((tf, ROW_TILES, LANES), lambda i, a, b: (i, 0, 0))],
            out_specs=pl.BlockSpec(memory_space=pl.ANY),
            scratch_shapes=[pltpu.VMEM((MOE_ROWS, ROW_TILES, LANES), U32),
                            pltpu.SemaphoreType.DMA(()), pltpu.SemaphoreType.DMA(())],
        ),
        out_shape=jax.ShapeDtypeStruct((n_rows, ROW_TILES, LANES), U32),
        compiler_params=_params(("arbitrary",)),
        name="dispatch",
    )(pend, padded, dest, h2p)


def _experts_body(be_ref, nu_ref, x_ref, wgu_ref, bgu_ref, wdn_ref, bdn_ref, o_ref):
    @pl.when(pl.program_id(0) < nu_ref[0])
    def _():
        xb = _load_packed_rows(x_ref).astype(BF16)
        width = 256
        acc = jnp.zeros((x_ref.shape[0], D_MODEL), F32)
        for c in range(D_EXPERT // width):
            glu_cols = pl.ds(c * width, width)
            lin_cols = pl.ds(D_EXPERT + c * width, width)
            glu = jnp.dot(xb, wgu_ref[:, glu_cols], preferred_element_type=F32) + bgu_ref[:, glu_cols]
            lin = jnp.dot(xb, wgu_ref[:, lin_cols], preferred_element_type=F32) + bgu_ref[:, lin_cols]
            glu = jnp.minimum(glu, SWIGLU_LIMIT)
            lin = jnp.clip(lin, -SWIGLU_LIMIT, SWIGLU_LIMIT)
            act = glu * jax.nn.sigmoid(SWIGLU_ALPHA * glu) * (lin + 1.0)
            acc = acc + jnp.dot(act.astype(BF16), wdn_ref[glu_cols, :], preferred_element_type=F32)
        _store_packed_rows(o_ref, acc + bdn_ref[...])


def _experts(blk_expert, n_used, xg, w_gu_bf, b_gu, w_dn_bf, b_dn):
    n_rows = xg.shape[0]
    n_blk = n_rows // MOE_ROWS

    def live(j, nu):
        return jnp.minimum(j, nu[0] - 1)

    return pl.pallas_call(
        _experts_body,
        grid_spec=pltpu.PrefetchScalarGridSpec(
            num_scalar_prefetch=2,
            grid=(n_blk,),
            in_specs=[
                pl.BlockSpec((MOE_ROWS, ROW_TILES, LANES), lambda j, be, nu: (live(j, nu), 0, 0)),
                pl.BlockSpec((None, D_MODEL, 2 * D_EXPERT), lambda j, be, nu: (be[live(j, nu)], 0, 0)),
                pl.BlockSpec((None, 1, 2 * D_EXPERT), lambda j, be, nu: (be[live(j, nu)], 0, 0)),
                pl.BlockSpec((None, D_EXPERT, D_MODEL), lambda j, be, nu: (be[live(j, nu)], 0, 0)),
                pl.BlockSpec((None, 1, D_MODEL), lambda j, be, nu: (be[live(j, nu)], 0, 0)),
            ],
            out_specs=pl.BlockSpec((MOE_ROWS, ROW_TILES, LANES), lambda j, be, nu: (live(j, nu), 0, 0)),
        ),
        out_shape=jax.ShapeDtypeStruct((n_rows, ROW_TILES, LANES), U32),
        compiler_params=_params(("arbitrary",)),
        name="experts",
    )(blk_expert, n_used, xg, w_gu_bf, b_gu, w_dn_bf, b_dn)


def _combine_body(dest_ref, x1_ref, gcol_ref, gf_ref, yb_ref, out_ref, rows_ref, sem):
    th = x1_ref.shape[0]

    def row_copy(r, kk):
        return pltpu.make_async_copy(yb_ref.at[dest_ref[kk, r]], rows_ref.at[kk, r], sem)

    def issue(r, carry):
        for kk in range(TOP_K):
            row_copy(r, kk).start()
        return carry

    def drain(r, carry):
        for kk in range(TOP_K):
            row_copy(r, kk).wait()
        return carry

    lax.fori_loop(0, th, issue, 0)
    lax.fori_loop(0, th, drain, 0)

    gates = gcol_ref[...]
    x2 = x1_ref[...]
    for kk in range(TOP_K):
        x2 = x2 + gates[:, kk:kk + 1] * _load_packed_rows(rows_ref.at[kk])
    ms = jnp.mean(x2 * x2, axis=-1, keepdims=True)
    out_ref[...] = x2 * lax.rsqrt(ms + NORM_EPS) * gf_ref[...]


def _combine(dest, x1, gcol, g_final, yb, th):
    T = x1.shape[0]
    return pl.pallas_call(
        _combine_body,
        grid=(T // th,),
        in_specs=[pl.BlockSpec((8, th), lambda i: (0, i), memory_space=pltpu.SMEM),
                  pl.BlockSpec((th, D_MODEL), lambda i: (i, 0)),
                  pl.BlockSpec((th, LANES), lambda i: (i, 0)),
                  pl.BlockSpec((1, D_MODEL), lambda i: (0, 0)),
                  pl.BlockSpec(memory_space=pl.ANY)],
        out_specs=pl.BlockSpec((th, D_MODEL), lambda i: (i, 0)),
        out_shape=jax.ShapeDtypeStruct((T, D_MODEL), F32),
        scratch_shapes=[pltpu.VMEM((TOP_K, th, ROW_TILES, LANES), U32), pltpu.SemaphoreType.DMA(())],
        compiler_params=_params(("arbitrary",)),
        name="combine_final",
    )(dest, x1, gcol, g_final, yb)


def _rotary_tables(seq):
    inv = ROPE_THETA ** (-jnp.arange(0, D_HEAD, 2, dtype=F32) / D_HEAD)
    ang = jnp.arange(seq, dtype=F32)[:, None] * inv[None, :]
    cos, sin = jnp.cos(ang), jnp.sin(ang)
    return jnp.concatenate([cos, cos], axis=1), jnp.concatenate([-sin, sin], axis=1)


def _retention_tables():
    C = RET_CHUNK
    log_g = jnp.log1p(-(2.0 ** (-5.0 - jnp.arange(N_HEADS, dtype=F32))))
    idx = jnp.arange(C, dtype=F32)
    diff = idx[:, None] - idx[None, :]
    decay = jnp.where(diff[None] >= 0, jnp.exp(jnp.maximum(diff, 0.0)[None] * log_g[:, None, None]), 0.0)
    zeta = jnp.exp((C - 1 - idx)[None, :] * log_g[:, None])
    xi = jnp.exp((idx + 1)[None, :] * log_g[:, None])
    wide = (N_HEADS, C, D_HEAD)
    return decay, jnp.broadcast_to(zeta[:, :, None], wide), jnp.broadcast_to(xi[:, :, None], wide)


def _layer(x2, batch, seq, norm_mix_g, w_in, w_out, norm_ffn_g, w_router, b_router,
           w_gate_up, b_gate_up, w_down, b_down, out_gain):
    T = x2.shape[0]
    tm = min(512, seq)
    cos_t, sin_t = _rotary_tables(seq)
    pr, pa = _in_proj(x2, norm_mix_g.reshape(1, D_MODEL), w_in.astype(BF16), cos_t, sin_t, seq, tm)

    ret = _retention(pr.reshape(AQ, batch, seq, D_HEAD), *_retention_tables(), tc=min(1024, seq))
    att = _attention(pa.reshape(N_COL_BLOCKS - AQ, batch, seq, D_HEAD))

    tri = (jnp.arange(tm)[:, None] < jnp.arange(tm)[None, :]).astype(BF16)
    x1, h2p, meta, gcol, counts = _out_proj(
        x2, ret.reshape(N_HEADS, T, D_HEAD), att.reshape(N_HEADS, T, D_HEAD),
        w_out.astype(BF16), norm_ffn_g.reshape(1, D_MODEL),
        w_router.T, b_router.reshape(N_EXPERTS, 1), tri, tm)

    cnt = counts[:, 0].astype(I32)
    padded = (cnt + MOE_ROWS - 1) // MOE_ROWS * MOE_ROWS
    pend = jnp.cumsum(padded).astype(I32)
    pstart = pend - padded
    n_rows = T * TOP_K + N_EXPERTS * MOE_ROWS
    n_blk = n_rows // MOE_ROWS
    blk_first_row = jnp.arange(n_blk, dtype=I32) * MOE_ROWS
    blk_expert = jnp.minimum(jnp.sum(pend[None, :] <= blk_first_row[:, None], axis=1), N_EXPERTS - 1).astype(I32)
    n_used = (pend[-1:] // MOE_ROWS).astype(I32)

    dest = _dest_rows(pstart, meta, te=min(4096, T))
    xg = _dispatch(pend, padded, dest, h2p, n_rows, tf=min(256, T))
    yb = _experts(blk_expert, n_used, xg, w_gate_up.astype(BF16),
                  b_gate_up.reshape(N_EXPERTS, 1, 2 * D_EXPERT), w_down.astype(BF16),
                  b_down.reshape(N_EXPERTS, 1, D_MODEL))
    return _combine(dest, x1, gcol, out_gain, yb, th=min(256, T))


def kernel(x, norm_mix_g, w_in, w_out, norm_ffn_g, w_router, b_router, w_gate_up, b_gate_up, w_down,
           b_down, norm_final_g):
    batch, seq, _ = x.shape
    depth = w_in.shape[0]
    assert depth == 1, "the final RMSNorm is fused into the last layer's combine kernel"
    out = _layer(x.reshape(batch * seq, D_MODEL), batch, seq, norm_mix_g[0], w_in[0], w_out[0],
                 norm_ffn_g[0], w_router[0], b_router[0], w_gate_up[0], b_gate_up[0], w_down[0],
                 b_down[0], norm_final_g.reshape(1, D_MODEL))
    return out.reshape(batch, seq, D_MODEL)
```
